```python
import jax, jax.numpy as jnp
from jax import lax
import numpy as np

D_MODEL = 1024
BATCH = 2
SEQ = 8192
DEPTH = 1

HEAD_DIM = 64
N_NSA_HEADS = 8
N_NSA_KV = 2
NSA_GROUP = N_NSA_HEADS // N_NSA_KV
N_FOX_HEADS = 8
D_NSA = N_NSA_HEADS * HEAD_DIM
D_FOX = N_FOX_HEADS * HEAD_DIM
D_MIX = D_NSA + D_FOX
D_KV = N_NSA_KV * HEAD_DIM
D_PROJ = D_NSA + 6 * D_KV + 3 * N_NSA_HEADS + 3 * D_FOX + N_FOX_HEADS
CMP_BLOCK = 32
CMP_STRIDE = 16
CMP_HIDDEN = 128
SEL_BLOCK = 64
SEL_TOPK = 16
WINDOW = 512
Q_BLOCK = 128
N_EXPERTS = 32
TOP_K = 4
D_FF = D_MODEL
SWIGLU_LIMIT = 7.0
SWIGLU_ALPHA = 1.702
PLE_DIM = 256
EPS = 1e-6
NEG = -1e30
FORCED_SCORE = 1e9

kernel_name = "hybrid_nsa_fox_moe_block"


def rmsnorm(x, g):
    xf = x.astype(jnp.float32)
    y = xf * lax.rsqrt(jnp.mean(xf * xf, axis=-1, keepdims=True) + EPS)
    return (y * g.astype(jnp.float32)).astype(x.dtype)


def masked_softmax(logits, mask):
    logits = jnp.where(mask, logits.astype(jnp.float32), NEG)
    m = jnp.max(logits, axis=-1, keepdims=True)
    e = jnp.exp(logits - m) * mask
    return e / jnp.maximum(jnp.sum(e, axis=-1, keepdims=True), 1e-30)


def alibi_slopes(n):
    return jnp.exp2(-8.0 * jnp.arange(1, n + 1, dtype=jnp.float32) / n)


def proj_split_points():
    sizes = [D_NSA, D_KV, D_KV, D_KV, D_KV, D_KV, D_KV, 3 * N_NSA_HEADS,
             D_FOX, D_FOX, D_FOX, N_FOX_HEADS]
    pts, acc = [], 0
    for s in sizes[:-1]:
        acc += s
        pts.append(acc)
    return pts


def compress(kv, w1, w2, pe):
    B, T, G, Dh = kv.shape
    nc = (T - CMP_BLOCK) // CMP_STRIDE + 1
    idx = jnp.arange(nc)[:, None] * CMP_STRIDE + jnp.arange(CMP_BLOCK)[None, :]
    blocks = kv[:, idx] + pe[None, None, :, None, :]
    blocks = jnp.swapaxes(blocks, 2, 3).reshape(B, nc, G, CMP_BLOCK * Dh)
    return jax.nn.gelu(blocks @ w1) @ w2


def nsa_attention(q, k_c, v_c, k_s, v_s, k_w, v_w, gate_logits, slopes):
    B, T, H, Dh = q.shape
    G = k_s.shape[2]
    R = H // G
    scale = Dh ** -0.5
    nc = k_c.shape[1]
    ns = T // SEL_BLOCK
    topk = min(SEL_TOPK, ns)
    cmp_start = jnp.arange(nc) * CMP_STRIDE
    cmp_end = cmp_start + CMP_BLOCK - 1
    sel_start = jnp.arange(ns) * SEL_BLOCK
    overlap = ((cmp_start[:, None] < sel_start[None, :] + SEL_BLOCK)
               & (cmp_end[:, None] >= sel_start[None, :])).astype(jnp.float32)
    ks_blocks = k_s.reshape(B, ns, SEL_BLOCK, G, Dh).transpose(0, 3, 1, 2, 4)
    vs_blocks = v_s.reshape(B, ns, SEL_BLOCK, G, Dh).transpose(0, 3, 1, 2, 4)
    pad = jnp.zeros((B, WINDOW, G, Dh), k_w.dtype)
    kw_pad = jnp.concatenate([pad, k_w], axis=1)
    vw_pad = jnp.concatenate([pad, v_w], axis=1)
    qg = q.reshape(B, T, G, R, Dh)
    gates = jax.nn.sigmoid(gate_logits.reshape(B, T, G, R, 3))
    sl = slopes.reshape(G, R)
    b_ix = jnp.arange(B)[:, None, None, None]
    g_ix = jnp.arange(G)[None, None, :, None]
    j_sel = jnp.arange(ns)
    l_sel = jnp.arange(SEL_BLOCK)

    def block(start):
        qb = lax.dynamic_slice_in_dim(qg, start, Q_BLOCK, 1)
        gb = lax.dynamic_slice_in_dim(gates, start, Q_BLOCK, 1)
        t = start + jnp.arange(Q_BLOCK)
        lc = jnp.einsum('bqgrd,bcgd->bqgrc', qb, k_c).astype(jnp.float32) * scale
        dist_c = (t[:, None] - cmp_end[None, :]).astype(jnp.float32)
        lc = lc - sl[None, None, :, :, None] * dist_c[None, :, None, None, :]
        mask_c = (cmp_end[None, :] <= t[:, None])[None, :, None, None, :]
        p_c = masked_softmax(lc, mask_c)
        o_c = jnp.einsum('bqgrc,bcgd->bqgrd', p_c.astype(v_c.dtype), v_c)
        imp = jnp.einsum('bqgrc,cn->bqgn', p_c, overlap)
        cur = t // SEL_BLOCK
        valid = j_sel[None, :] <= cur[:, None]
        forced = valid & ((j_sel[None, :] == 0) | (j_sel[None, :] == cur[:, None])
                          | (j_sel[None, :] == cur[:, None] - 1))
        score = jnp.where(forced[None, :, None, :], FORCED_SCORE,
                          jnp.where(valid[None, :, None, :], imp, -1.0))
        _, idx = lax.top_k(score, topk)
        ks = ks_blocks[b_ix, g_ix, idx]
        vs = vs_blocks[b_ix, g_ix, idx]
        ls = jnp.einsum('bqgrd,bqgkld->bqgrkl', qb, ks).astype(jnp.float32) * scale
        pos = idx[..., None] * SEL_BLOCK + l_sel
        dist_s = (t[None, :, None, None, None] - pos)[:, :, :, None]
        ls = ls - sl[None, None, :, :, None, None] * dist_s.astype(jnp.float32)
        mask_s = jnp.broadcast_to(dist_s >= 0, ls.shape)
        flat = (B, Q_BLOCK, G, R, topk * SEL_BLOCK)
        p_s = masked_softmax(ls.reshape(flat), mask_s.reshape(flat)).reshape(ls.shape)
        o_s = jnp.einsum('bqgrkl,bqgkld->bqgrd', p_s.astype(vs.dtype), vs)
        kw = lax.dynamic_slice_in_dim(kw_pad, start, Q_BLOCK + WINDOW, 1)
        vw = lax.dynamic_slice_in_dim(vw_pad, start, Q_BLOCK + WINDOW, 1)
        s = start - WINDOW + jnp.arange(Q_BLOCK + WINDOW)
        dist_w = t[:, None] - s[None, :]
        lw = jnp.einsum('bqgrd,bsgd->bqgrs', qb, kw).astype(jnp.float32) * scale
        lw = lw - sl[None, None, :, :, None] * dist_w.astype(jnp.float32)[None, :, None, None, :]
        mask_w = ((dist_w >= 0) & (dist_w < WINDOW) & (s[None, :] >= 0))[None, :, None, None, :]
        p_w = masked_softmax(lw, mask_w)
        o_w = jnp.einsum('bqgrs,bsgd->bqgrd', p_w.astype(vw.dtype), vw)
        o = gb[..., 0:1] * o_c + gb[..., 1:2] * o_s + gb[..., 2:3] * o_w
        return o.reshape(B, Q_BLOCK, H * Dh)

    starts = jnp.arange(T // Q_BLOCK) * Q_BLOCK
    out = lax.map(block, starts)
    return out.transpose(1, 0, 2, 3).reshape(B, T, H * Dh)


def fox_attention(q, k, v, f_logit):
    B, T, H, Dh = q.shape
    scale = Dh ** -0.5
    c = jnp.cumsum(jax.nn.log_sigmoid(f_logit.astype(jnp.float32)), axis=1).transpose(0, 2, 1)
    kpos = jnp.arange(T)

    def block(start):
        qb = lax.dynamic_slice_in_dim(q, start, Q_BLOCK, 1)
        cb = lax.dynamic_slice_in_dim(c, start, Q_BLOCK, 2)
        t = start + jnp.arange(Q_BLOCK)
        l = jnp.einsum('bqhd,bshd->bhqs', qb, k).astype(jnp.float32) * scale
        l = l + (cb[..., None] - c[:, :, None, :])
        mask = (kpos[None, :] <= t[:, None])[None, None]
        pr = masked_softmax(l, mask)
        o = jnp.einsum('bhqs,bshd->bqhd', pr.astype(v.dtype), v)
        return o.reshape(B, Q_BLOCK, H * Dh)

    starts = jnp.arange(T // Q_BLOCK) * Q_BLOCK
    out = lax.map(block, starts)
    return out.transpose(1, 0, 2, 3).reshape(B, T, H * Dh)


def moe(x, w_router, b_router, w_up, b_up, w_down, b_down):
    B, T, D = x.shape
    xf = x.reshape(-1, D)
    n = xf.shape[0]
    logits = (xf @ w_router + b_router).astype(jnp.float32)
    top_v, top_i = lax.top_k(logits, TOP_K)
    w = jax.nn.softmax(top_v, axis=-1)
    flat_e = top_i.reshape(-1)
    order = jnp.argsort(flat_e)
    sorted_e = flat_e[order]
    tok = order // TOP_K
    sizes = jnp.bincount(flat_e, length=N_EXPERTS).astype(jnp.int32)
    xs = xf[tok]
    h = lax.ragged_dot(xs, w_up, sizes) + b_up[sorted_e]
    g, lin = jnp.split(h, 2, axis=-1)
    g = jnp.minimum(g, SWIGLU_LIMIT)
    lin = jnp.clip(lin, -SWIGLU_LIMIT, SWIGLU_LIMIT)
    act = g * jax.nn.sigmoid(SWIGLU_ALPHA * g) * (lin + 1.0)
    y = lax.ragged_dot(act, w_down, sizes) + b_down[sorted_e]
    y = y * w.reshape(-1)[order][:, None].astype(y.dtype)
    return jax.ops.segment_sum(y, tok, num_segments=n).reshape(B, T, D)


def setup_inputs(seed: int = 0) -> dict:
    key = jax.random.key(seed)
    ks = jax.random.split(key, 32)
    L, D, E, F = DEPTH, D_MODEL, N_EXPERTS, D_FF
    nrm = lambda k, shape, fan: jax.random.normal(k, shape, jnp.float32) * (fan ** -0.5)
    gain = lambda k, shape: 1.0 + 0.02 * jax.random.normal(k, shape, jnp.float32)
    small = lambda k, shape, s: s * jax.random.normal(k, shape, jnp.float32)
    return {
        "x": jax.random.normal(ks[0], (BATCH, SEQ, D), jnp.float32),
        "p": jax.random.normal(ks[1], (L, BATCH, SEQ, PLE_DIM), jnp.float32),
        "ln1": gain(ks[2], (L, D)),
        "w_in": nrm(ks[3], (L, D, D_PROJ), D),
        "b_fg": jax.random.uniform(ks[4], (L, N_FOX_HEADS), jnp.float32, 1.0, 5.0),
        "w_cmp1_k": nrm(ks[5], (L, CMP_BLOCK * HEAD_DIM, CMP_HIDDEN), CMP_BLOCK * HEAD_DIM),
        "w_cmp2_k": nrm(ks[6], (L, CMP_HIDDEN, HEAD_DIM), CMP_HIDDEN),
        "pe_cmp_k": small(ks[7], (L, CMP_BLOCK, HEAD_DIM), 0.1),
        "w_cmp1_v": nrm(ks[8], (L, CMP_BLOCK * HEAD_DIM, CMP_HIDDEN), CMP_BLOCK * HEAD_DIM),
        "w_cmp2_v": nrm(ks[9], (L, CMP_HIDDEN, HEAD_DIM), CMP_HIDDEN),
        "pe_cmp_v": small(ks[10], (L, CMP_BLOCK, HEAD_DIM), 0.1),
        "gn_nsa": gain(ks[11], (L, D_NSA)),
        "gn_fox": gain(ks[12], (L, D_FOX)),
        "w_out": nrm(ks[13], (L, D_MIX, D), D_MIX),
        "ln2": gain(ks[14], (L, D)),
        "w_router": nrm(ks[15], (L, D, E), D),
        "b_router": small(ks[16], (L, E), 0.01),
        "w_up": nrm(ks[17], (L, E, D, 2 * F), D),
        "b_up": small(ks[18], (L, E, 2 * F), 0.01),
        "w_down": nrm(ks[19], (L, E, F, D), F),
        "b_down": small(ks[20], (L, E, D), 0.01),
        "ln_ple": gain(ks[21], (L, D)),
        "w_ple": nrm(ks[22], (L, PLE_DIM, D), PLE_DIM),
        "w_ple_gate": nrm(ks[23], (L, D, D), D),
        "ln_f": gain(ks[24], (D,)),
    }


def reference(x, p, ln1, w_in, b_fg, w_cmp1_k, w_cmp2_k, pe_cmp_k, w_cmp1_v, w_cmp2_v,
              pe_cmp_v, gn_nsa, gn_fox, w_out, ln2, w_router, b_router, w_up, b_up,
              w_down, b_down, ln_ple, w_ple, w_ple_gate, ln_f):
    B, T, _ = x.shape
    G, Dh = N_NSA_KV, HEAD_DIM
    slopes = alibi_slopes(N_NSA_HEADS)
    pts = proj_split_points()
    h = x
    for i in range(DEPTH):
        u = rmsnorm(h, ln1[i])
        proj = u @ w_in[i]
        (q_n, k_c, v_c, k_s, v_s, k_w, v_w, g_n,
         q_f, k_f, v_f, f_f) = jnp.split(proj, pts, axis=-1)
        kc = compress(k_c.reshape(B, T, G, Dh), w_cmp1_k[i], w_cmp2_k[i], pe_cmp_k[i])
        vc = compress(v_c.reshape(B, T, G, Dh), w_cmp1_v[i], w_cmp2_v[i], pe_cmp_v[i])
        o_nsa = nsa_attention(q_n.reshape(B, T, N_NSA_HEADS, Dh), kc, vc,
                              k_s.reshape(B, T, G, Dh), v_s.reshape(B, T, G, Dh),
                              k_w.reshape(B, T, G, Dh), v_w.reshape(B, T, G, Dh),
                              g_n.reshape(B, T, N_NSA_HEADS, 3), slopes)
        o_fox = fox_attention(q_f.reshape(B, T, N_FOX_HEADS, Dh),
                              k_f.reshape(B, T, N_FOX_HEADS, Dh),
                              v_f.reshape(B, T, N_FOX_HEADS, Dh),
                              f_f + b_fg[i])
        mix = jnp.concatenate([rmsnorm(o_nsa, gn_nsa[i]), rmsnorm(o_fox, gn_fox[i])], axis=-1)
        h = h + mix @ w_out[i]
        h = h + moe(rmsnorm(h, ln2[i]), w_router[i], b_router[i], w_up[i], b_up[i],
                    w_down[i], b_down[i])
        h = h + (p[i] @ w_ple[i]) * jax.nn.sigmoid(rmsnorm(h, ln_ple[i]) @ w_ple_gate[i])
    return rmsnorm(h, ln_f)
```

```python
import functools

import jax
import jax.numpy as jnp
from jax import lax
from jax.experimental import pallas as pl
from jax.experimental.pallas import tpu as pltpu

F32 = jnp.float32
BF16 = jnp.bfloat16
I32 = jnp.int32

D_MODEL = 1024
HEAD_DIM = 64
N_NSA_HEADS = 8
N_NSA_KV = 2
NSA_GROUP = N_NSA_HEADS // N_NSA_KV
N_FOX_HEADS = 8
D_NSA = N_NSA_HEADS * HEAD_DIM
D_FOX = N_FOX_HEADS * HEAD_DIM
D_KV = N_NSA_KV * HEAD_DIM
CMP_BLOCK = 32
CMP_STRIDE = 16
CMP_HIDDEN = 128
SEL_BLOCK = 64
SEL_TOPK = 16
WINDOW = 512
NSA_Q = 128
N_EXPERTS = 32
TOP_K = 4
D_FF = D_MODEL
SWIGLU_LIMIT = 7.0
SWIGLU_ALPHA = 1.702
PLE_DIM = 256
EPS = 1e-6
NEG = -1e30
FORCED_SCORE = 1e9
QK_SCALE = HEAD_DIM ** -0.5

GATE_PAD = 16
VMEM_LIMIT = 56 * 1024 * 1024

PROJ_TM = 512
FCUM_TC = 512
FOX_TQ = 512
NSA_TK = 256
MIX_TM = 512
POS_TR = 512
DISPATCH_TD = 256
GMM_TM = 256
FINAL_TC = 256


def _iota(shape, dim):
    return lax.broadcasted_iota(I32, shape, dim)


def _rms(v, g):
    return v * lax.rsqrt(jnp.mean(v * v, axis=-1, keepdims=True) + EPS) * g


def _dot(a, b):
    return jnp.dot(a, b, preferred_element_type=F32)


def _dot_nt(a, b):
    return lax.dot_general(a, b, (((1,), (1,)), ((), ())), preferred_element_type=F32)


def _split_bf16(v):
    hi = v.astype(BF16)
    lo = (v - hi.astype(F32)).astype(BF16)
    return hi, lo


def _masked_softmax(logits, mask):
    lg = jnp.where(mask, logits, NEG)
    m = jnp.max(lg, axis=-1, keepdims=True)
    e = jnp.where(mask, jnp.exp(lg - m), 0.0)
    return e / jnp.maximum(jnp.sum(e, axis=-1, keepdims=True), 1e-30)


def _proj_kernel(x_ref, ln_ref, wm_ref, ws_ref, qn_ref, kcr_ref, vcr_ref, ks_ref, vs_ref,
                 kw_ref, vw_ref, qf_ref, kf_ref, vf_ref, gate_ref, ff_ref):
    ub = _rms(x_ref[0], ln_ref[...]).astype(BF16)

    qn_ref[0] = (_dot(ub, wm_ref[:, 0:D_NSA]) * QK_SCALE).astype(BF16)
    off = D_NSA
    kv = _dot(ub, wm_ref[:, off:off + 6 * D_KV])
    for n, (ref, dt) in enumerate(((kcr_ref, F32), (vcr_ref, F32), (ks_ref, BF16),
                                   (vs_ref, BF16), (kw_ref, BF16), (vw_ref, BF16))):
        for g in range(N_NSA_KV):
            lo = n * D_KV + g * HEAD_DIM
            ref[0, g] = kv[:, lo:lo + HEAD_DIM].astype(dt)
    off += 6 * D_KV
    for ref, sc in ((qf_ref, QK_SCALE), (kf_ref, 1.0), (vf_ref, 1.0)):
        r = _dot(ub, wm_ref[:, off:off + D_FOX])
        for h in range(N_FOX_HEADS):
            ref[0, h] = (r[:, h * HEAD_DIM:(h + 1) * HEAD_DIM] * sc).astype(BF16)
        off += D_FOX
    s = _dot(ub, ws_ref[...])
    for g in range(N_NSA_KV):
        gate_ref[0, g] = s[:, g * GATE_PAD:(g + 1) * GATE_PAD]
    ff_ref[0] = s[:, N_NSA_KV * GATE_PAD:N_NSA_KV * GATE_PAD + N_FOX_HEADS]


def _proj(x, ln1, w_main, w_small):
    B, T, D = x.shape
    tm = PROJ_TM
    G, H, Dh = N_NSA_KV, N_FOX_HEADS, HEAD_DIM
    grp = lambda dt: jax.ShapeDtypeStruct((B, G, T, Dh), dt)
    hd = jax.ShapeDtypeStruct((B, H, T, Dh), BF16)
    grp_spec = pl.BlockSpec((1, G, tm, Dh), lambda b, i: (b, 0, i, 0))
    hd_spec = pl.BlockSpec((1, H, tm, Dh), lambda b, i: (b, 0, i, 0))
    full = lambda a: pl.BlockSpec(a.shape, lambda b, i: (0,) * a.ndim)
    return pl.pallas_call(
        _proj_kernel,
        grid=(B, T // tm),
        in_specs=[pl.BlockSpec((1, tm, D), lambda b, i: (b, i, 0)), full(ln1), full(w_main),
                  full(w_small)],
        out_specs=[pl.BlockSpec((1, tm, D_NSA), lambda b, i: (b, i, 0)),
                   grp_spec, grp_spec, grp_spec, grp_spec, grp_spec, grp_spec,
                   hd_spec, hd_spec, hd_spec,
                   pl.BlockSpec((1, G, tm, GATE_PAD), lambda b, i: (b, 0, i, 0)),
                   pl.BlockSpec((1, tm, N_FOX_HEADS), lambda b, i: (b, i, 0))],
        out_shape=[jax.ShapeDtypeStruct((B, T, D_NSA), BF16),
                   grp(F32), grp(F32), grp(BF16), grp(BF16), grp(BF16), grp(BF16),
                   hd, hd, hd,
                   jax.ShapeDtypeStruct((B, G, T, GATE_PAD), F32),
                   jax.ShapeDtypeStruct((B, T, N_FOX_HEADS), F32)],
        compiler_params=pltpu.CompilerParams(
            dimension_semantics=("parallel", "parallel"), vmem_limit_bytes=VMEM_LIMIT),
        name="proj",
    )(x, ln1, w_main, w_small)


def _compress_kernel(kx_ref, vx_ref, pek_ref, pev_ref, w1k_ref, w2k_ref, w1v_ref, w2v_ref,
                     kc_ref, vc_ref):
    half = CMP_STRIDE * HEAD_DIM

    def one(x_ref, pe_ref, w1_ref, w2_ref, o_ref):
        x = x_ref[0, 0]
        n = x.shape[0]
        a = _dot((x + pe_ref[0:1, :]).astype(BF16), w1_ref[0:half, :])
        b = _dot((x + pe_ref[1:2, :]).astype(BF16), w1_ref[half:2 * half, :])
        hdn = a + pltpu.roll(b, n - 1, 0)
        o_ref[0, 0] = _dot(jax.nn.gelu(hdn).astype(BF16), w2_ref[...]).astype(BF16)

    one(kx_ref, pek_ref, w1k_ref, w2k_ref, kc_ref)
    one(vx_ref, pev_ref, w1v_ref, w2v_ref, vc_ref)


def _compress(kc_raw, vc_raw, pek, pev, w1k, w2k, w1v, w2v):
    B, G, T, Dh = kc_raw.shape
    ncp = T // CMP_STRIDE
    kx = kc_raw.reshape(B, G, ncp, CMP_STRIDE * Dh)
    vx = vc_raw.reshape(B, G, ncp, CMP_STRIDE * Dh)
    full = lambda a: pl.BlockSpec(a.shape, lambda b, g: (0,) * a.ndim)
    xspec = pl.BlockSpec((1, 1, ncp, CMP_STRIDE * Dh), lambda b, g: (b, g, 0, 0))
    ospec = pl.BlockSpec((1, 1, ncp, Dh), lambda b, g: (b, g, 0, 0))
    osd = jax.ShapeDtypeStruct((B, G, ncp, Dh), BF16)
    return pl.pallas_call(
        _compress_kernel,
        grid=(B, G),
        in_specs=[xspec, xspec, full(pek), full(pev), full(w1k), full(w2k), full(w1v), full(w2v)],
        out_specs=[ospec, ospec],
        out_shape=[osd, osd],
        compiler_params=pltpu.CompilerParams(
            dimension_semantics=("parallel", "parallel"), vmem_limit_bytes=VMEM_LIMIT),
        name="compress",
    )(kx, vx, pek, pev, w1k, w2k, w1v, w2v)


def _fcum_kernel(ff_ref, b_ref, c_ref, carry_ref):
    @pl.when(pl.program_id(1) == 0)
    def _():
        carry_ref[...] = jnp.zeros_like(carry_ref)

    f = ff_ref[0] + b_ref[...]
    ls = jnp.minimum(f, 0.0) - jnp.log1p(jnp.exp(-jnp.abs(f)))
    tc = f.shape[0]
    tri = jnp.where(_iota((tc, tc), 0) >= _iota((tc, tc), 1), 1.0, 0.0).astype(BF16)
    hi = ls.astype(BF16)
    mid, lo = _split_bf16(ls - hi.astype(F32))
    cs = _dot(tri, hi) + _dot(tri, mid) + _dot(tri, lo) + carry_ref[...]
    c_ref[0] = cs
    carry_ref[...] = cs[tc - 1:tc, :]


def _fcum(ff, b_fg):
    B, T, H = ff.shape
    tc = FCUM_TC
    return pl.pallas_call(
        _fcum_kernel,
        grid=(B, T // tc),
        in_specs=[pl.BlockSpec((1, tc, H), lambda b, i: (b, i, 0)),
                  pl.BlockSpec((1, H), lambda b, i: (0, 0))],
        out_specs=pl.BlockSpec((1, tc, H), lambda b, i: (b, i, 0)),
        out_shape=jax.ShapeDtypeStruct((B, T, H), F32),
        scratch_shapes=[pltpu.VMEM((1, H), F32)],
        compiler_params=pltpu.CompilerParams(dimension_semantics=("parallel", "arbitrary")),
        name="fcum",
    )(ff, b_fg)


def _fox_kernel(q_ref, k_ref, v_ref, ccol_ref, crow_ref, o_ref, *, tq, tk):
    hp = pl.program_id(1)
    qi = pl.program_id(2)
    start = qi * tq
    t_idx = start + _iota((tq, 1), 0)
    lane_h = _iota((tq, N_FOX_HEADS), 1)
    n_kt = (start + tq + tk - 1) // tk
    outs = []
    for hh in range(2):
        q = q_ref[0, hh]
        c_t = jnp.sum(jnp.where(lane_h == hp * 2 + hh, ccol_ref[0], 0.0), axis=1, keepdims=True)

        def body(kt, carry, hh=hh, q=q, c_t=c_t):
            m, l, acc = carry
            k0 = pl.multiple_of(kt * tk, tk)
            k = k_ref[0, hh, pl.ds(k0, tk), :]
            v = v_ref[0, hh, pl.ds(k0, tk), :]
            c_s = crow_ref[0, hh, :, pl.ds(k0, tk)]
            lg = _dot_nt(q, k) + (c_t - c_s)
            s_idx = k0 + _iota((1, tk), 1)
            lg = jnp.where(s_idx <= t_idx, lg, NEG)
            m_new = jnp.maximum(m, jnp.max(lg, axis=-1, keepdims=True))
            alpha = jnp.exp(m - m_new)
            p = jnp.exp(lg - m_new)
            l = alpha * l + jnp.sum(p, axis=-1, keepdims=True)
            acc = alpha * acc + _dot(p.astype(BF16), v)
            return m_new, l, acc

        init = (jnp.full((tq, 1), NEG, F32), jnp.zeros((tq, 1), F32),
                jnp.zeros((tq, HEAD_DIM), F32))
        _, l, acc = lax.fori_loop(0, n_kt, body, init)
        outs.append(acc / jnp.maximum(l, 1e-30))
    o_ref[0] = jnp.concatenate(outs, axis=-1)


def _fox(qf, kf, vf, c_col, c_row):
    B, H, T, Dh = qf.shape
    tq = FOX_TQ
    return pl.pallas_call(
        functools.partial(_fox_kernel, tq=tq, tk=tq),
        grid=(B, H // 2, T // tq),
        in_specs=[pl.BlockSpec((1, 2, tq, Dh), lambda b, h, i: (b, h, i, 0)),
                  pl.BlockSpec((1, 2, T, Dh), lambda b, h, i: (b, h, 0, 0)),
                  pl.BlockSpec((1, 2, T, Dh), lambda b, h, i: (b, h, 0, 0)),
                  pl.BlockSpec((1, tq, H), lambda b, h, i: (b, i, 0)),
                  pl.BlockSpec((1, 2, 1, T), lambda b, h, i: (b, h, 0, 0))],
        out_specs=pl.BlockSpec((1, tq, 2 * Dh), lambda b, h, i: (b, i, h)),
        out_shape=jax.ShapeDtypeStruct((B, T, H * Dh), F32),
        compiler_params=pltpu.CompilerParams(
            dimension_semantics=("parallel", "parallel", "parallel"),
            vmem_limit_bytes=VMEM_LIMIT),
        name="fox",
    )(qf, kf, vf, c_col, c_row)


def _nsa_kernel(slopes_ref, q_ref, gate_ref, kc_ref, vc_ref, ks_ref, vs_ref, kw_ref, vw_ref,
                o_ref, *, T, tk):
    g = pl.program_id(1)
    qi = pl.program_id(2)
    Q, R = NSA_Q, NSA_GROUP
    ns = T // SEL_BLOCK
    ncp = T // CMP_STRIDE
    start = qi * Q

    qb = q_ref[0]
    q2 = jnp.concatenate([qb[:, r * HEAD_DIM:(r + 1) * HEAD_DIM] for r in range(R)], axis=0)
    t2 = start + jnp.bitwise_and(_iota((R * Q, 1), 0), Q - 1)
    slope = jnp.concatenate(
        [jnp.full((Q, 1), slopes_ref[g * R + r], F32) for r in range(R)], axis=0)

    cmp_end = _iota((1, ncp), 1) * CMP_STRIDE + (CMP_BLOCK - 1)
    lc = _dot_nt(q2, kc_ref[0, 0]) - slope * (t2 - cmp_end).astype(F32)
    p_c = _masked_softmax(lc, cmp_end <= t2)
    o_c = _dot(p_c.astype(BF16), vc_ref[0, 0])

    pg = p_c[0:Q] + p_c[Q:2 * Q] + p_c[2 * Q:3 * Q] + p_c[3 * Q:4 * Q]
    c0 = _iota((ncp, ns), 0) * CMP_STRIDE
    n0 = _iota((ncp, ns), 1) * SEL_BLOCK
    overlap = jnp.where((c0 < n0 + SEL_BLOCK) & (c0 + (CMP_BLOCK - 1) >= n0), 1.0, 0.0).astype(BF16)
    pg_hi, pg_lo = _split_bf16(pg)
    imp = _dot(pg_hi, overlap) + _dot(pg_lo, overlap)

    cur = jnp.right_shift(start + _iota((Q, 1), 0), 6)
    j = _iota((Q, ns), 1)
    valid = j <= cur
    forced = valid & ((j == 0) | (j == cur) | (j == cur - 1))
    score = jnp.where(forced, FORCED_SCORE, jnp.where(valid, imp, -1.0))
    jf = j.astype(F32)
    sel = jnp.zeros((Q, ns), F32)
    for _ in range(min(SEL_TOPK, ns)):
        mx = jnp.max(score, axis=-1, keepdims=True)
        first = jnp.min(jnp.where(score == mx, jf, float(ns)), axis=-1, keepdims=True)
        hit = jf == first
        sel = jnp.where(hit, 1.0, sel)
        score = jnp.where(hit, -2.0, score)
    sel_b = sel.astype(BF16)

    def sel_body(kt, carry):
        m, l, acc = carry
        k0 = pl.multiple_of(kt * tk, tk)
        k = ks_ref[0, 0, pl.ds(k0, tk), :]
        v = vs_ref[0, 0, pl.ds(k0, tk), :]
        expand = jnp.where(
            _iota((ns, tk), 0) == jnp.right_shift(k0 + _iota((ns, tk), 1), 6), 1.0, 0.0
        ).astype(BF16)
        key_sel = _dot(sel_b, expand)
        key_sel = jnp.concatenate([key_sel] * R, axis=0)
        dist = t2 - (k0 + _iota((1, tk), 1))
        lg = _dot_nt(q2, k) - slope * dist.astype(F32)
        lg = jnp.where((key_sel > 0.5) & (dist >= 0), lg, NEG)
        m_new = jnp.maximum(m, jnp.max(lg, axis=-1, keepdims=True))
        alpha = jnp.exp(m - m_new)
        p = jnp.exp(lg - m_new)
        l = alpha * l + jnp.sum(p, axis=-1, keepdims=True)
        acc = alpha * acc + _dot(p.astype(BF16), v)
        return m_new, l, acc

    init = (jnp.full((R * Q, 1), NEG, F32), jnp.zeros((R * Q, 1), F32),
            jnp.zeros((R * Q, HEAD_DIM), F32))
    _, l_s, acc_s = lax.fori_loop(0, (start + Q + tk - 1) // tk, sel_body, init)
    o_s = acc_s / jnp.maximum(l_s, 1e-30)

    wk = WINDOW + Q
    w0 = pl.multiple_of(jnp.maximum(start - WINDOW, 0), Q)
    dist = t2 - (w0 + _iota((1, wk), 1))
    lw = _dot_nt(q2, kw_ref[0, 0, pl.ds(w0, wk), :]) - slope * dist.astype(F32)
    p_w = _masked_softmax(lw, (dist >= 0) & (dist < WINDOW))
    o_w = _dot(p_w.astype(BF16), vw_ref[0, 0, pl.ds(w0, wk), :])

    sg = jax.nn.sigmoid(gate_ref[0, 0])
    outs = []
    for r in range(R):
        rows = slice(r * Q, (r + 1) * Q)
        outs.append(sg[:, r:r + 1] * o_c[rows] + sg[:, R + r:R + r + 1] * o_s[rows]
                    + sg[:, 2 * R + r:2 * R + r + 1] * o_w[rows])
    o_ref[0] = jnp.concatenate(outs, axis=-1)


def _nsa(slopes, qn, gates, kc, vc, ks, vs, kw, vw):
    B, T, _ = qn.shape
    G, Dh, Q = N_NSA_KV, HEAD_DIM, NSA_Q
    ncp = kc.shape[2]
    tk = min(NSA_TK, T)
    kv_spec = pl.BlockSpec((1, 1, T, Dh), lambda b, g, i: (b, g, 0, 0))
    c_spec = pl.BlockSpec((1, 1, ncp, Dh), lambda b, g, i: (b, g, 0, 0))
    return pl.pallas_call(
        functools.partial(_nsa_kernel, T=T, tk=tk),
        grid=(B, G, T // Q),
        in_specs=[pl.BlockSpec(memory_space=pltpu.SMEM),
                  pl.BlockSpec((1, Q, NSA_GROUP * Dh), lambda b, g, i: (b, i, g)),
                  pl.BlockSpec((1, 1, Q, GATE_PAD), lambda b, g, i: (b, g, i, 0)),
                  c_spec, c_spec, kv_spec, kv_spec, kv_spec, kv_spec],
        out_specs=pl.BlockSpec((1, Q, NSA_GROUP * Dh), lambda b, g, i: (b, i, g)),
        out_shape=jax.ShapeDtypeStruct((B, T, D_NSA), F32),
        compiler_params=pltpu.CompilerParams(
            dimension_semantics=("parallel", "parallel", "parallel"),
            vmem_limit_bytes=VMEM_LIMIT),
        name="nsa",
    )(slopes, qn, gates, kc, vc, ks, vs, kw, vw)


def _mix_kernel(on_ref, of_ref, x_ref, gnn_ref, gnf_ref, wo_ref, ln2_ref, wrh_ref, wrl_ref,
                br_ref, h1_ref, u2_ref, ti_ref, tw_ref, cnt_ref):
    @pl.when(pl.program_id(0) == 0)
    def _():
        cnt_ref[...] = jnp.zeros_like(cnt_ref)

    mn = _rms(on_ref[...], gnn_ref[...]).astype(BF16)
    mf = _rms(of_ref[...], gnf_ref[...]).astype(BF16)
    h1 = x_ref[...] + _dot(mn, wo_ref[0:D_NSA, :]) + _dot(mf, wo_ref[D_NSA:D_NSA + D_FOX, :])
    h1_ref[...] = h1
    u2 = _rms(h1, ln2_ref[...])
    u2_ref[...] = u2

    uh, ul = _split_bf16(u2)
    logits = (_dot(uh, wrh_ref[...]) + _dot(ul, wrh_ref[...]) + _dot(uh, wrl_ref[...])
              + br_ref[...])
    tm = logits.shape[0]
    ef = _iota((tm, N_EXPERTS), 1).astype(F32)
    lane_k = _iota((tm, TOP_K), 1)
    vals = []
    ti = jnp.zeros((tm, TOP_K), F32)
    cnt = jnp.zeros((tm, N_EXPERTS), F32)
    s = logits
    for k in range(TOP_K):
        mx = jnp.max(s, axis=-1, keepdims=True)
        first = jnp.min(jnp.where(s == mx, ef, float(N_EXPERTS)), axis=-1, keepdims=True)
        hit = ef == first
        vals.append(mx)
        ti = jnp.where(lane_k == k, first, ti)
        cnt = jnp.where(hit, 1.0, cnt)
        s = jnp.where(hit, NEG, s)
    es = [jnp.exp(v - vals[0]) for v in vals]
    den = es[0] + es[1] + es[2] + es[3]
    tw = jnp.zeros((tm, TOP_K), F32)
    for k in range(TOP_K):
        tw = jnp.where(lane_k == k, es[k] / den, tw)
    ti_ref[...] = ti.astype(I32)
    tw_ref[...] = tw
    cnt_ref[...] += jnp.sum(cnt, axis=0, keepdims=True)


def _mix(o_nsa, o_fox, x2, gnn, gnf, w_out, ln2, wr_hi, wr_lo, b_router):
    N, D = x2.shape
    tm = MIX_TM
    row = lambda w: pl.BlockSpec((tm, w), lambda i: (i, 0))
    full = lambda a: pl.BlockSpec(a.shape, lambda i: (0,) * a.ndim)
    return pl.pallas_call(
        _mix_kernel,
        grid=(N // tm,),
        in_specs=[row(D_NSA), row(D_FOX), row(D), full(gnn), full(gnf), full(w_out), full(ln2),
                  full(wr_hi), full(wr_lo), full(b_router)],
        out_specs=[row(D), row(D), row(TOP_K), row(TOP_K),
                   pl.BlockSpec((1, N_EXPERTS), lambda i: (0, 0))],
        out_shape=[jax.ShapeDtypeStruct((N, D), F32), jax.ShapeDtypeStruct((N, D), F32),
                   jax.ShapeDtypeStruct((N, TOP_K), I32), jax.ShapeDtypeStruct((N, TOP_K), F32),
                   jax.ShapeDtypeStruct((1, N_EXPERTS), F32)],
        compiler_params=pltpu.CompilerParams(
            dimension_semantics=("arbitrary",), vmem_limit_bytes=VMEM_LIMIT),
        name="mix",
    )(o_nsa, o_fox, x2, gnn, gnf, w_out, ln2, wr_hi, wr_lo, b_router)


def _pos_kernel(ti_ref, off_ref, pos_ref, run_ref):
    @pl.when(pl.program_id(0) == 0)
    def _():
        run_ref[...] = jnp.zeros_like(run_ref)

    ti = ti_ref[...]
    tr = ti.shape[0]
    e = _iota((tr, N_EXPERTS), 1)
    hits = [jnp.where(ti[:, k:k + 1] == e, 1.0, 0.0) for k in range(TOP_K)]
    cnt = hits[0] + hits[1] + hits[2] + hits[3]
    below = jnp.where(_iota((tr, tr), 0) > _iota((tr, tr), 1), 1.0, 0.0).astype(BF16)
    base = off_ref[...] + run_ref[...] + _dot(below, cnt.astype(BF16))
    lane_k = _iota((tr, TOP_K), 1)
    pos = jnp.zeros((tr, TOP_K), F32)
    for k in range(TOP_K):
        pos = jnp.where(lane_k == k, jnp.sum(hits[k] * base, axis=-1, keepdims=True), pos)
    pos_ref[...] = pos.astype(I32)
    run_ref[...] += jnp.sum(cnt, axis=0, keepdims=True)


def _positions(top_i, offsets):
    N = top_i.shape[0]
    tr = POS_TR
    return pl.pallas_call(
        _pos_kernel,
        grid=(N // tr,),
        in_specs=[pl.BlockSpec((tr, TOP_K), lambda i: (i, 0)),
                  pl.BlockSpec((1, N_EXPERTS), lambda i: (0, 0))],
        out_specs=pl.BlockSpec((tr, TOP_K), lambda i: (i, 0)),
        out_shape=jax.ShapeDtypeStruct((N, TOP_K), I32),
        scratch_shapes=[pltpu.VMEM((1, N_EXPERTS), F32)],
        compiler_params=pltpu.CompilerParams(dimension_semantics=("arbitrary",)),
        name="pos",
    )(top_i, offsets)


def _row_copy(src_ref, src_row, dst_ref, dst_row, sem):
    return pltpu.make_async_copy(src_ref.at[pl.ds(src_row, 1)], dst_ref.at[pl.ds(dst_row, 1)], sem)


def _dispatch_kernel(pos_ref, u_ref, init_ref, xs_ref, sem, *, td):
    del init_ref

    def start(i, c):
        for k in range(TOP_K):
            _row_copy(u_ref, i, xs_ref, pos_ref[i * TOP_K + k], sem).start()
        return c

    def wait(i, c):
        for k in range(TOP_K):
            _row_copy(u_ref, i, xs_ref, pos_ref[i * TOP_K + k], sem).wait()
        return c

    lax.fori_loop(0, td, start, 0)
    lax.fori_loop(0, td, wait, 0)


def _dispatch(pos_flat, u2, rows):
    N, D = u2.shape
    td = DISPATCH_TD
    init = jnp.zeros((rows, D), F32)
    return pl.pallas_call(
        functools.partial(_dispatch_kernel, td=td),
        grid=(N // td,),
        in_specs=[pl.BlockSpec((td * TOP_K,), lambda i: (i,), memory_space=pltpu.SMEM),
                  pl.BlockSpec((td, D), lambda i: (i, 0)),
                  pl.BlockSpec(memory_space=pl.ANY)],
        out_specs=pl.BlockSpec(memory_space=pl.ANY),
        out_shape=jax.ShapeDtypeStruct((rows, D), F32),
        scratch_shapes=[pltpu.SemaphoreType.DMA(())],
        input_output_aliases={2: 0},
        compiler_params=pltpu.CompilerParams(dimension_semantics=("arbitrary",)),
        name="dispatch",
    )(pos_flat, u2, init)


def _gmm_kernel(te_ref, nt_ref, x_ref, wu_ref, bu_ref, wd_ref, bd_ref, y_ref):
    @pl.when(pl.program_id(0) < nt_ref[0])
    def _():
        h = _dot(x_ref[...].astype(BF16), wu_ref[0]) + bu_ref[0]
        gate = jnp.minimum(h[:, :D_FF], SWIGLU_LIMIT)
        lin = jnp.clip(h[:, D_FF:], -SWIGLU_LIMIT, SWIGLU_LIMIT)
        act = gate * jax.nn.sigmoid(SWIGLU_ALPHA * gate) * (lin + 1.0)
        y_ref[...] = _dot(act.astype(BF16), wd_ref[0]) + bd_ref[0]

    @pl.when(pl.program_id(0) >= nt_ref[0])
    def _():
        y_ref[...] = jnp.zeros_like(y_ref)


def _gmm(tile_expert, n_tiles, xs, w_up, b_up, w_down, b_down):
    rows, D = xs.shape
    tm = GMM_TM
    E, _, F2 = w_up.shape
    row_map = lambda i, te, nt: (jnp.minimum(i, nt[0] - 1), 0)
    exp_map = lambda i, te, nt: (te[i], 0, 0)
    return pl.pallas_call(
        _gmm_kernel,
        grid_spec=pltpu.PrefetchScalarGridSpec(
            num_scalar_prefetch=2,
            grid=(rows // tm,),
            in_specs=[pl.BlockSpec((tm, D), row_map),
                      pl.BlockSpec((1, D, F2), exp_map),
                      pl.BlockSpec((1, 1, F2), exp_map),
                      pl.BlockSpec((1, F2 // 2, D), exp_map),
                      pl.BlockSpec((1, 1, D), exp_map)],
            out_specs=pl.BlockSpec((tm, D), lambda i, te, nt: (i, 0))),
        out_shape=jax.ShapeDtypeStruct((rows, D), F32),
        compiler_params=pltpu.CompilerParams(
            dimension_semantics=("arbitrary",), vmem_limit_bytes=VMEM_LIMIT),
        name="gmm",
    )(tile_expert, n_tiles, xs, w_up, b_up, w_down, b_down)


def _final_kernel(pos_ref, y_ref, tw_ref, h1_ref, p_ref, lnp_ref, wp_ref, wg_ref, lnf_ref,
                  o_ref, buf, sem, *, tc, last_layer):
    def start(i, c):
        for k in range(TOP_K):
            _row_copy(y_ref, pos_ref[i * TOP_K + k], buf.at[k], i, sem).start()
        return c

    def wait(i, c):
        for k in range(TOP_K):
            _row_copy(y_ref, pos_ref[i * TOP_K + k], buf.at[k], i, sem).wait()
        return c

    lax.fori_loop(0, tc, start, 0)
    lax.fori_loop(0, tc, wait, 0)

    tw = tw_ref[...]
    moe = tw[:, 0:1] * buf[0]
    for k in range(1, TOP_K):
        moe = moe + tw[:, k:k + 1] * buf[k]
    h2 = h1_ref[...] + moe
    gate = jax.nn.sigmoid(_dot(_rms(h2, lnp_ref[...]).astype(BF16), wg_ref[...]))
    h3 = h2 + _dot(p_ref[...].astype(BF16), wp_ref[...]) * gate
    o_ref[...] = _rms(h3, lnf_ref[...]) if last_layer else h3


def _final(pos_flat, y, top_w, h1, p2, ln_ple, w_ple, w_gate, ln_f, last_layer):
    N, D = h1.shape
    tc = FINAL_TC
    row = lambda w: pl.BlockSpec((tc, w), lambda i: (i, 0))
    full = lambda a: pl.BlockSpec(a.shape, lambda i: (0,) * a.ndim)
    return pl.pallas_call(
        functools.partial(_final_kernel, tc=tc, last_layer=last_layer),
        grid=(N // tc,),
        in_specs=[pl.BlockSpec((tc * TOP_K,), lambda i: (i,), memory_space=pltpu.SMEM),
                  pl.BlockSpec(memory_space=pl.ANY),
                  row(TOP_K), row(D), row(PLE_DIM), full(ln_ple), full(w_ple), full(w_gate),
                  full(ln_f)],
        out_specs=row(D),
        out_shape=jax.ShapeDtypeStruct((N, D), F32),
        scratch_shapes=[pltpu.VMEM((TOP_K, tc, D), F32), pltpu.SemaphoreType.DMA(())],
        compiler_params=pltpu.CompilerParams(
            dimension_semantics=("arbitrary",), vmem_limit_bytes=VMEM_LIMIT),
        name="final",
    )(pos_flat, y, top_w, h1, p2, ln_ple, w_ple, w_gate, ln_f)


def _proj_weights(w_in):
    sizes = [D_NSA] + [D_KV] * 6 + [3 * N_NSA_HEADS, D_FOX, D_FOX, D_FOX, N_FOX_HEADS]
    offs = [0]
    for s in sizes:
        offs.append(offs[-1] + s)
    col = lambda n: w_in[:, offs[n]:offs[n + 1]]
    w_main = jnp.concatenate([col(n) for n in (0, 1, 2, 3, 4, 5, 6, 8, 9, 10)], axis=1)
    wg = col(7).reshape(-1, N_NSA_KV, NSA_GROUP, 3).transpose(0, 1, 3, 2)
    wg = wg.reshape(-1, N_NSA_KV, 3 * NSA_GROUP)
    wg = jnp.pad(wg, ((0, 0), (0, 0), (0, GATE_PAD - 3 * NSA_GROUP))).reshape(-1, N_NSA_KV * GATE_PAD)
    w_small = jnp.concatenate([wg, col(11)], axis=1)
    w_small = jnp.pad(w_small, ((0, 0), (0, 128 - w_small.shape[1])))
    return w_main.astype(BF16), w_small.astype(BF16)


def _expert_tiles(counts, n_tiles_max):
    tm = GMM_TM
    cnt = counts.reshape(-1).astype(I32)
    padded = (cnt + tm - 1) // tm * tm
    ends = jnp.cumsum(padded)
    offsets = (ends - padded).astype(F32).reshape(1, -1)
    tile_expert = jnp.searchsorted(ends // tm, jnp.arange(n_tiles_max, dtype=I32), side="right")
    tile_expert = jnp.minimum(tile_expert, N_EXPERTS - 1).astype(I32)
    n_tiles = (ends[-1:] // tm).astype(I32)
    return offsets, tile_expert, n_tiles


def kernel(x, p, ln1, w_in, b_fg, w_cmp1_k, w_cmp2_k, pe_cmp_k, w_cmp1_v, w_cmp2_v, pe_cmp_v,
           gn_nsa, gn_fox, w_out, ln2, w_router, b_router, w_up, b_up, w_down, b_down, ln_ple,
           w_ple, w_ple_gate, ln_f):
    B, T, D = x.shape
    N = B * T
    depth = ln1.shape[0]
    slopes = jnp.exp2(-8.0 * jnp.arange(1, N_NSA_HEADS + 1, dtype=F32) / N_NSA_HEADS)
    row2 = lambda v: v.reshape(1, -1)
    half_pe = lambda pe: pe.reshape(2, CMP_STRIDE * HEAD_DIM)

    h = x
    for i in range(depth):
        w_main, w_small = _proj_weights(w_in[i])
        (qn, kc_raw, vc_raw, ks, vs, kw, vw, qf, kf, vf, gates, ff) = _proj(
            h, row2(ln1[i]), w_main, w_small)
        kc, vc = _compress(kc_raw, vc_raw, half_pe(pe_cmp_k[i]), half_pe(pe_cmp_v[i]),
                           w_cmp1_k[i].astype(BF16), w_cmp2_k[i].astype(BF16),
                           w_cmp1_v[i].astype(BF16), w_cmp2_v[i].astype(BF16))
        o_nsa = _nsa(slopes, qn, gates, kc, vc, ks, vs, kw, vw)
        c_col = _fcum(ff, row2(b_fg[i]))
        c_row = c_col.transpose(0, 2, 1).reshape(B, N_FOX_HEADS, 1, T)
        o_fox = _fox(qf, kf, vf, c_col, c_row)

        wr_hi = w_router[i].astype(BF16)
        wr_lo = (w_router[i] - wr_hi.astype(F32)).astype(BF16)
        h1, u2, top_i, top_w, counts = _mix(
            o_nsa.reshape(N, D_NSA), o_fox.reshape(N, D_FOX), h.reshape(N, D),
            row2(gn_nsa[i]), row2(gn_fox[i]), w_out[i].astype(BF16), row2(ln2[i]),
            wr_hi, wr_lo, row2(b_router[i]))

        n_tiles_max = N * TOP_K // GMM_TM + N_EXPERTS
        offsets, tile_expert, n_tiles = _expert_tiles(counts, n_tiles_max)
        pos_flat = _positions(top_i, offsets).reshape(N * TOP_K)
        xs = _dispatch(pos_flat, u2, n_tiles_max * GMM_TM)
        y = _gmm(tile_expert, n_tiles, xs, w_up[i].astype(BF16),
                 b_up[i].reshape(N_EXPERTS, 1, -1), w_down[i].astype(BF16),
                 b_down[i].reshape(N_EXPERTS, 1, -1))
        h = _final(pos_flat, y, top_w, h1, p[i].reshape(N, PLE_DIM), row2(ln_ple[i]),
                   w_ple[i].astype(BF16), w_ple_gate[i].astype(BF16), row2(ln_f),
                   last_layer=(i == depth - 1)).reshape(B, T, D)
    return h
```
